```python
import jax
import jax.numpy as jnp
from jax import lax
import numpy as np

D_MODEL = 1024
BATCH = 2
SEQ = 8192
DEPTH = 4
DEC_BATCH = 128
DEC_SEQ = 4
PAST_LEN = 2048
PAGE_SIZE = 128

N_MIXERS = 2
D_INNER = D_MODEL
HEAD_DIM = 64
N_HEADS = D_INNER // HEAD_DIM
N_RWKV = (DEPTH + 1) // 2
N_SB = DEPTH // 2
LORA_DECAY = 64
LORA_AAA = 64
LORA_MV = 32
D_PLE = 256
Q_BLOCK = 128
NORM_EPS = 1e-6
GN_EPS = 64e-5
L2_EPS = 1e-12

kernel_name = 'rwkv7_stickbreaking_hybrid_step'


def rms_norm(x, g):
    x32 = x.astype(jnp.float32)
    y = x32 * lax.rsqrt(jnp.mean(x32 * x32, axis=-1, keepdims=True) + NORM_EPS)
    return y.astype(x.dtype) * g


def rwkv7_mixer(xn, shift_prev, wkv0, v_first, mix, w_rkvz, w0, w1, w2, a0, a1, a2,
                k_k, k_a, r_k, lnx_w, lnx_b, vres):
    B, T, _ = xn.shape
    x_prev = jnp.concatenate([shift_prev[:, None, :].astype(xn.dtype), xn[:, :-1]], axis=1)
    xs = xn[None] + (x_prev - xn)[None] * mix[:, None, None, :]
    r, k, v, z = jnp.einsum('nbtd,nde->nbte', xs[:4], w_rkvz)
    w = -jax.nn.softplus(-(w0 + jnp.tanh(xs[4] @ w1) @ w2)) - 0.5
    decay = jnp.exp(-jnp.exp(w.astype(jnp.float32)))
    a = jax.nn.sigmoid(a0 + (xs[5] @ a1) @ a2)
    v_own = v
    if vres is not None:
        v0, v1, v2 = vres
        v = v + (v_first - v) * jax.nn.sigmoid(v0 + (xs[2] @ v1) @ v2)
    hs = lambda t: t.reshape(B, T, N_HEADS, HEAD_DIM).astype(jnp.float32)
    kk = hs(k * k_k)
    kk = kk / jnp.maximum(jnp.sqrt(jnp.sum(kk * kk, axis=-1, keepdims=True)), L2_EPS)
    k = k * (1.0 + (a - 1.0) * k_a)
    rh, kh, vh, ah, dh = hs(r), hs(k), hs(v), hs(a), hs(decay)

    def step(S, inp):
        r_t, w_t, k_t, v_t, kk_t, a_t = inp
        sa = jnp.einsum('bhij,bhj->bhi', S, -kk_t)
        S = (S * w_t[:, :, None, :] + sa[..., None] * (kk_t * a_t)[:, :, None, :]
             + v_t[..., None] * k_t[:, :, None, :])
        return S, jnp.einsum('bhij,bhj->bhi', S, r_t)

    seq = tuple(jnp.moveaxis(t, 1, 0) for t in (rh, dh, kh, vh, kk, ah))
    S, ys = lax.scan(step, wkv0.astype(jnp.float32), seq)
    y = jnp.moveaxis(ys, 0, 1)
    mu = jnp.mean(y, axis=-1, keepdims=True)
    var = jnp.mean(jnp.square(y - mu), axis=-1, keepdims=True)
    yn = ((y - mu) * lax.rsqrt(var + GN_EPS)).reshape(B, T, D_INNER) * lnx_w + lnx_b
    bonus = (jnp.sum(rh * kh * r_k, axis=-1, keepdims=True) * vh).reshape(B, T, D_INNER)
    out = (yn + bonus).astype(xn.dtype) * jax.nn.silu(z)
    return out, S.astype(wkv0.dtype), xn[:, -1], v_own


def sb_weights(logits, mask):
    u = jnp.where(mask, jax.nn.log_sigmoid(-logits), 0.0)
    suffix = lax.cumsum(u, axis=u.ndim - 1, reverse=True) - u
    return jnp.where(mask, jnp.exp(jax.nn.log_sigmoid(logits) + suffix), 0.0)


def sb_project(xn, w_qkvz):
    B, T, _ = xn.shape
    q, k, v, z = jnp.split(xn @ w_qkvz, 4, axis=-1)
    hs = lambda t: t.reshape(B, T, N_HEADS, HEAD_DIM)
    return hs(q), hs(k), hs(v), z


def sb_prompt(xn, w_qkvz, bias):
    B, T, _ = xn.shape
    q, k, v, z = sb_project(xn, w_qkvz)
    nb = T // Q_BLOCK
    qb = jnp.swapaxes(q.reshape(B, nb, Q_BLOCK, N_HEADS, HEAD_DIM), 0, 1)
    starts = jnp.arange(nb, dtype=jnp.int32) * Q_BLOCK
    k_pos = jnp.arange(T, dtype=jnp.int32)
    scale = HEAD_DIM ** -0.5
    b_h = bias.astype(jnp.float32)[None, :, None, None]

    def block(args):
        q_blk, q0 = args
        logits = jnp.einsum('bqhd,bkhd->bhqk', q_blk, k).astype(jnp.float32) * scale + b_h
        t_pos = q0 + jnp.arange(Q_BLOCK, dtype=jnp.int32)
        mask = k_pos[None, :] < t_pos[:, None]
        A = sb_weights(logits, mask).astype(v.dtype)
        return jnp.einsum('bhqk,bkhd->bqhd', A, v)

    o = lax.map(block, (qb, starts))
    o = jnp.swapaxes(o, 0, 1).reshape(B, T, D_INNER)
    return o * jax.nn.silu(z), k, v


def sb_sample(xn, w_qkvz, bias, past_k, past_v):
    B, T, _ = xn.shape
    P = past_k.shape[1]
    q, k, v, z = sb_project(xn, w_qkvz)
    scale = HEAD_DIM ** -0.5
    b_h = bias.astype(jnp.float32)[None, :, None, None]
    logits = jnp.concatenate([jnp.einsum('bqhd,bkhd->bhqk', q, past_k),
                              jnp.einsum('bqhd,bkhd->bhqk', q, k)], axis=-1).astype(jnp.float32) * scale + b_h
    t_pos = P + jnp.arange(T, dtype=jnp.int32)
    k_pos = jnp.arange(P + T, dtype=jnp.int32)
    mask = k_pos[None, :] < t_pos[:, None]
    A = sb_weights(logits, mask).astype(v.dtype)
    o = (jnp.einsum('bhqk,bkhd->bqhd', A[..., :P], past_v)
         + jnp.einsum('bhqk,bkhd->bqhd', A[..., P:], v))
    return o.reshape(B, T, D_INNER) * jax.nn.silu(z), k, v


def setup_inputs(seed: int = 0) -> dict:
    key = jax.random.key(seed)
    ks = iter(jax.random.split(key, 48))
    nrm = lambda shape, scale: jax.random.normal(next(ks), shape, jnp.float32) * scale
    n_pages = PAST_LEN // PAGE_SIZE
    n_used = DEC_BATCH * n_pages
    n_phys = n_used + n_used // 4
    page_table = jax.random.permutation(next(ks), n_phys)[:n_used].reshape(DEC_BATCH, n_pages).astype(jnp.int32)
    return {
        'x_prompt': nrm((BATCH, SEQ, D_MODEL), 1.0),
        'x_sample': nrm((DEC_BATCH, DEC_SEQ, D_MODEL), 1.0),
        'cache_k': nrm((N_SB, n_phys, PAGE_SIZE, N_HEADS, HEAD_DIM), 1.0),
        'cache_v': nrm((N_SB, n_phys, PAGE_SIZE, N_HEADS, HEAD_DIM), 1.0),
        'state_wkv': nrm((N_RWKV, DEC_BATCH, N_HEADS, HEAD_DIM, HEAD_DIM), 0.3),
        'state_shift': nrm((N_RWKV, DEC_BATCH, D_MODEL), 1.0),
        'page_table': page_table,
        'p_prompt': nrm((DEPTH, BATCH, SEQ, D_PLE), 1.0),
        'p_sample': nrm((DEPTH, DEC_BATCH, DEC_SEQ, D_PLE), 1.0),
        'norm_pre': 1.0 + nrm((DEPTH, D_MODEL), 0.02),
        'norm_post': 1.0 + nrm((DEPTH, D_MODEL), 0.02),
        'w_out': nrm((DEPTH, D_INNER, D_MODEL), D_INNER ** -0.5),
        'w_ple': nrm((DEPTH, D_PLE, D_MODEL), D_PLE ** -0.5),
        'w_ple_gate': nrm((DEPTH, D_MODEL, D_MODEL), D_MODEL ** -0.5),
        'rwkv_mix': jax.random.uniform(next(ks), (N_RWKV, 6, D_MODEL), jnp.float32),
        'w_rkvz': nrm((N_RWKV, 4, D_MODEL, D_INNER), D_MODEL ** -0.5),
        'w_decay0': jax.random.uniform(next(ks), (N_RWKV, D_INNER), jnp.float32, -6.5, -1.5),
        'w_decay1': nrm((N_RWKV, D_MODEL, LORA_DECAY), D_MODEL ** -0.5),
        'w_decay2': nrm((N_RWKV, LORA_DECAY, D_INNER), 0.1 * LORA_DECAY ** -0.5),
        'w_a0': nrm((N_RWKV, D_INNER), 0.1),
        'w_a1': nrm((N_RWKV, D_MODEL, LORA_AAA), D_MODEL ** -0.5),
        'w_a2': nrm((N_RWKV, LORA_AAA, D_INNER), 0.1 * LORA_AAA ** -0.5),
        'w_v0': 1.0 + nrm((N_RWKV - 1, D_INNER), 0.1),
        'w_v1': nrm((N_RWKV - 1, D_MODEL, LORA_MV), D_MODEL ** -0.5),
        'w_v2': nrm((N_RWKV - 1, LORA_MV, D_INNER), 0.1 * LORA_MV ** -0.5),
        'k_k': 0.85 + nrm((N_RWKV, D_INNER), 0.02),
        'k_a': 1.0 + nrm((N_RWKV, D_INNER), 0.02),
        'r_k': nrm((N_RWKV, N_HEADS, HEAD_DIM), 0.1),
        'lnx_w': 1.0 + nrm((N_RWKV, D_INNER), 0.02),
        'lnx_b': nrm((N_RWKV, D_INNER), 0.01),
        'w_qkvz': nrm((N_SB, D_MODEL, 4 * D_INNER), D_MODEL ** -0.5),
        'sb_bias': jax.random.uniform(next(ks), (N_SB, N_HEADS), jnp.float32, -7.0, -5.0),
    }


def reference(x_prompt, x_sample, cache_k, cache_v, state_wkv, state_shift, page_table,
              p_prompt, p_sample, norm_pre, norm_post, w_out, w_ple, w_ple_gate,
              rwkv_mix, w_rkvz, w_decay0, w_decay1, w_decay2, w_a0, w_a1, w_a2,
              w_v0, w_v1, w_v2, k_k, k_a, r_k, lnx_w, lnx_b, w_qkvz, sb_bias):
    hp, hs = x_prompt, x_sample
    bp_ = x_prompt.shape[0]
    db = page_table.shape[0]
    kp_l, vp_l, ks_l, vs_l = [], [], [], []
    wkvp_l, shp_l, wkvs_l, shs_l = [], [], [], []
    vf_p = vf_s = None
    for i in range(DEPTH):
        j = i // N_MIXERS
        xp = rms_norm(hp, norm_pre[i])
        xs = rms_norm(hs, norm_pre[i])
        if i % N_MIXERS == 0:
            vres = None if j == 0 else (w_v0[j - 1], w_v1[j - 1], w_v2[j - 1])
            prm = (rwkv_mix[j], w_rkvz[j], w_decay0[j], w_decay1[j], w_decay2[j],
                   w_a0[j], w_a1[j], w_a2[j], k_k[j], k_a[j], r_k[j], lnx_w[j], lnx_b[j])
            zero_shift = jnp.zeros((bp_, D_MODEL), xp.dtype)
            zero_wkv = jnp.zeros((bp_, N_HEADS, HEAD_DIM, HEAD_DIM), state_wkv.dtype)
            br_p, S_p, last_p, v_p = rwkv7_mixer(xp, zero_shift, zero_wkv, vf_p, *prm, vres)
            br_s, S_s, last_s, v_s = rwkv7_mixer(xs, state_shift[j], state_wkv[j], vf_s, *prm, vres)
            if j == 0:
                vf_p, vf_s = v_p, v_s
            wkvp_l.append(S_p); shp_l.append(last_p)
            wkvs_l.append(S_s); shs_l.append(last_s)
        else:
            br_p, k_p, v_p = sb_prompt(xp, w_qkvz[j], sb_bias[j])
            past_k = cache_k[j][page_table].reshape(db, -1, N_HEADS, HEAD_DIM)
            past_v = cache_v[j][page_table].reshape(db, -1, N_HEADS, HEAD_DIM)
            br_s, k_s, v_s = sb_sample(xs, w_qkvz[j], sb_bias[j], past_k, past_v)
            kp_l.append(k_p); vp_l.append(v_p)
            ks_l.append(k_s); vs_l.append(v_s)
        hp = hp + rms_norm(br_p @ w_out[i], norm_post[i])
        hs = hs + rms_norm(br_s @ w_out[i], norm_post[i])
        hp = hp + (p_prompt[i] @ w_ple[i]) * jax.nn.sigmoid(hp @ w_ple_gate[i])
        hs = hs + (p_sample[i] @ w_ple[i]) * jax.nn.sigmoid(hs @ w_ple_gate[i])
    y_prompt, y_sample = hp, hs
    k_prompt, v_prompt = jnp.stack(kp_l), jnp.stack(vp_l)
    wkv_prompt, shift_prompt = jnp.stack(wkvp_l), jnp.stack(shp_l)
    k_sample, v_sample = jnp.stack(ks_l), jnp.stack(vs_l)
    wkv_sample, shift_sample = jnp.stack(wkvs_l), jnp.stack(shs_l)
    return (y_prompt, y_sample, k_prompt, v_prompt, wkv_prompt, shift_prompt,
            k_sample, v_sample, wkv_sample, shift_sample)
```

```python
import functools

import jax
import jax.numpy as jnp
import numpy as np
from jax import lax
from jax.experimental import pallas as pl
from jax.experimental.pallas import tpu as pltpu

HEAD_DIM = 64
NORM_EPS = 1e-6
GN_EPS = 64e-5
L2_EPS = 1e-12

LANES = 128
SUBLANES = 8
MXU_DIM = 256
VMEM_LIMIT = 56 * 1024 * 1024

SAMPLE_PAD = SUBLANES
HEADS_PER_GROUP = MXU_DIM // HEAD_DIM
F32 = jnp.float32
BF16 = jnp.bfloat16


def _bf(x):
    return x.astype(BF16)


def _dot(a, b):
    return jnp.dot(a, b, preferred_element_type=F32)


def _dot_nt(a, b):
    return lax.dot_general(a, b, (((1,), (1,)), ((), ())), preferred_element_type=F32)


def _dot_tn(a, b):
    return lax.dot_general(a, b, (((0,), (0,)), ((), ())), preferred_element_type=F32)


def _split2(x):
    hi = _bf(x)
    lo = _bf(x - hi.astype(F32))
    return hi, lo


def _dot_split(x, m_bf16):
    hi, lo = _split2(x)
    return _dot(hi, m_bf16) + _dot(lo, m_bf16)


def _rms(x, g):
    return x * lax.rsqrt(jnp.mean(x * x, axis=-1, keepdims=True) + NORM_EPS) * g


def _sigmoid(x):
    return 1.0 / (1.0 + jnp.exp(-x))


def _softplus(x):
    return jnp.maximum(x, 0.0) + jnp.log1p(jnp.exp(-jnp.abs(x)))


def _const_spec(shape):
    nd = len(shape)
    return pl.BlockSpec(shape, lambda *_: (0,) * nd)


def _params(sem):
    return pltpu.CompilerParams(dimension_semantics=sem, vmem_limit_bytes=VMEM_LIMIT)


def _rwkv_proj_kernel(*refs, tm, seq_rows, has_vres):
    it = iter(refs)
    h_ref = next(it); shift_ref = next(it)
    vfirst_ref = next(it) if has_vres else None
    npre_ref = next(it); mix_ref = next(it); wrkvz_ref = next(it)
    w0_ref = next(it); w1_ref = next(it); w2_ref = next(it)
    a0_ref = next(it); a1_ref = next(it); a2_ref = next(it)
    if has_vres:
        v0_ref = next(it); v1_ref = next(it); v2_ref = next(it)
    kk_ref = next(it); ka_ref = next(it); rk_ref = next(it); g_ref = next(it)
    r_o = next(it); lw_o = next(it); k_o = next(it); v_o = next(it); kkn_o = next(it)
    a_o = next(it); bonus_o = next(it); gz_o = next(it); xlast_o = next(it)
    carry = next(it)

    i = pl.program_id(0)

    @pl.when(i == 0)
    def _():
        carry[...] = jnp.zeros_like(carry)

    xn = _rms(h_ref[...], npre_ref[...])
    row = lax.broadcasted_iota(jnp.int32, (tm, 1), 0)
    rolled = pltpu.roll(xn, 1, 0)
    if seq_rows >= tm:
        tiles_per_seq = seq_rows // tm
        b = i // tiles_per_seq
        first = (i % tiles_per_seq) == 0
        prev = jnp.where(first, shift_ref[pl.ds(b, 1), :], carry[0:1, :])
        x_prev = jnp.where(row == 0, prev, rolled)
        carry[0:1, :] = xn[tm - 1:tm, :]
        xlast_o[...] = xn[tm - SUBLANES:tm, :]
    else:
        x_prev = jnp.where(row % seq_rows == 0, shift_ref[...], rolled)
        xlast_o[...] = xn
    dx = x_prev - xn

    def mixed(n):
        return xn + dx * mix_ref[n:n + 1, :]

    r = _dot(_bf(mixed(0)), wrkvz_ref[0])
    k = _dot(_bf(mixed(1)), wrkvz_ref[1])
    xv = _bf(mixed(2))
    v = _dot(xv, wrkvz_ref[2])
    z = _dot(_bf(mixed(3)), wrkvz_ref[3])
    wl = w0_ref[...] + _dot(_bf(jnp.tanh(_dot(_bf(mixed(4)), w1_ref[...]))), w2_ref[...])
    w = -_softplus(-wl) - 0.5
    lw_o[...] = -jnp.exp(w)
    a = _sigmoid(a0_ref[...] + _dot(_bf(_dot(_bf(mixed(5)), a1_ref[...])), a2_ref[...]))
    if has_vres:
        gate = _sigmoid(v0_ref[...] + _dot(_bf(_dot(xv, v1_ref[...])), v2_ref[...]))
        v = v + (vfirst_ref[...] - v) * gate
    kk = k * kk_ref[...]
    nrm = jnp.sqrt(_dot_split(kk * kk, g_ref[...]))
    kkn_o[...] = kk / jnp.maximum(nrm, L2_EPS)
    k = k * (1.0 + (a - 1.0) * ka_ref[...])
    bonus_o[...] = _dot_split(r * k * rk_ref[...], g_ref[...]) * v
    r_o[...] = r
    k_o[...] = k
    v_o[...] = v
    a_o[...] = a
    gz_o[...] = z * _sigmoid(z)


def _rwkv_proj(h2d, shift, vfirst, npre, mix, wrkvz, w0, w1, w2, a0, a1, a2, vres, k_k, k_a, r_k,
               gmat, *, seq_rows, tm):
    rows, d = h2d.shape
    has_vres = vres is not None
    grid = (rows // tm,)
    tile = pl.BlockSpec((tm, d), lambda i: (i, 0))
    row1 = lambda x: x.reshape(1, -1)
    ins = [h2d, shift]
    specs = [tile, tile if seq_rows < tm else _const_spec(shift.shape)]
    if has_vres:
        ins.append(vfirst); specs.append(tile)
    small = [row1(npre), mix, wrkvz, row1(w0), w1, w2, row1(a0), a1, a2]
    if has_vres:
        small += [row1(vres[0]), vres[1], vres[2]]
    small += [row1(k_k), row1(k_a), row1(r_k), gmat]
    ins += small
    specs += [_const_spec(s.shape) for s in small]
    if seq_rows >= tm:
        n_seq = rows // seq_rows
        tps = seq_rows // tm
        xlast_shape = jax.ShapeDtypeStruct((n_seq * SUBLANES, d), F32)
        xlast_spec = pl.BlockSpec((SUBLANES, d), lambda i: (i // tps, 0))
    else:
        xlast_shape = jax.ShapeDtypeStruct((rows, d), F32)
        xlast_spec = tile
    out_shape = [jax.ShapeDtypeStruct((rows, d), F32)] * 8 + [xlast_shape]
    out_specs = [tile] * 8 + [xlast_spec]
    return pl.pallas_call(
        functools.partial(_rwkv_proj_kernel, tm=tm, seq_rows=seq_rows, has_vres=has_vres),
        grid=grid, in_specs=specs, out_specs=out_specs, out_shape=out_shape,
        scratch_shapes=[pltpu.VMEM((SUBLANES, d), F32)],
        compiler_params=_params(("arbitrary",)),
    )(*ins)


def _rwkv_chunk_kernel(r_ref, lw_ref, k_ref, v_ref, kk_ref, a_ref, s0_ref, y_ref, sout_ref, state,
                       *, l_in, l_c, valid, n_groups, n_seq):
    c_idx = pl.program_id(1)
    gw = HEADS_PER_GROUP * HEAD_DIM
    rows = HEADS_PER_GROUP * l_c
    units = [(bi, g) for bi in range(n_seq) for g in range(n_groups)]

    @pl.when(c_idx == 0)
    def _():
        state[...] = jnp.zeros_like(state)
        for bi, g in units:
            for hh in range(HEADS_PER_GROUP):
                lo = hh * HEAD_DIM
                state[bi * n_groups + g, lo:lo + HEAD_DIM, lo:lo + HEAD_DIM] = (
                    s0_ref[bi, g * HEADS_PER_GROUP + hh])

    row_t = lax.broadcasted_iota(jnp.int32, (l_c, 1), 0)
    tri_incl = _bf(jnp.where(lax.broadcasted_iota(jnp.int32, (l_c, l_c), 1) <= row_t, 1.0, 0.0))
    ri = lax.broadcasted_iota(jnp.int32, (rows, rows), 0)
    ci = lax.broadcasted_iota(jnp.int32, (rows, rows), 1)
    same_head = (ri // l_c) == (ci // l_c)
    m_strict = same_head & (ci < ri)
    m_incl = same_head & (ci <= ri)
    lane_head = lax.broadcasted_iota(jnp.int32, (1, gw), 1) // HEAD_DIM
    vi = lax.broadcasted_iota(jnp.int32, (gw, gw), 0) // HEAD_DIM
    ki = lax.broadcasted_iota(jnp.int32, (gw, gw), 1) // HEAD_DIM
    m_state = vi == ki
    eye = jnp.where(ri == ci, 1.0, 0.0)

    def load(ref, bi, sl):
        x = ref[bi, :, sl]
        if valid < l_in:
            x = jnp.where(lax.broadcasted_iota(jnp.int32, (l_in, 1), 0) < valid, x, 0.0)
        if l_c > l_in:
            x = jnp.concatenate([x, jnp.zeros((l_c - l_in, gw), F32)], axis=0)
        return x

    def stack_masked(x):
        return _bf(jnp.concatenate(
            [jnp.where(lane_head == hh, x, 0.0) for hh in range(HEADS_PER_GROUP)], axis=0))

    def stack_tiled(x):
        return _bf(jnp.concatenate([x] * HEADS_PER_GROUP, axis=0))

    for bi, g in units:
        sl = slice(g * gw, (g + 1) * gw)
        si = bi * n_groups + g
        lw = load(lw_ref, bi, sl)
        hi = _bf(lw)
        rem = lw - hi.astype(F32)
        mid = _bf(rem)
        lo3 = _bf(rem - mid.astype(F32))
        c = _dot(tri_incl, hi) + _dot(tri_incl, mid) + _dot(tri_incl, lo3)
        c_last = c[l_c - 1:l_c, :]
        e_pos = jnp.exp(c)
        e_neg = jnp.exp(-c)
        e_prev = jnp.exp(c - lw)
        e_tail = jnp.exp(c_last - c)
        kkq = load(kk_ref, bi, sl)
        kq = load(k_ref, bi, sl)
        beta = kkq * load(a_ref, bi, sl)
        ah = stack_masked(-kkq * e_prev)
        rh = stack_masked(load(r_ref, bi, sl) * e_pos)
        bh = stack_tiled(beta * e_neg)
        kh = stack_tiled(kq * e_neg)
        bt = stack_tiled(beta * e_tail)
        kt = stack_tiled(kq * e_tail)
        vs = stack_masked(load(v_ref, bi, sl))

        a_ab = jnp.where(m_strict, _dot_nt(ah, bh), 0.0)
        a_ak = jnp.where(m_strict, _dot_nt(ah, kh), 0.0)
        a_rb = jnp.where(m_incl, _dot_nt(rh, bh), 0.0)
        a_rk = jnp.where(m_incl, _dot_nt(rh, kh), 0.0)

        tinv = eye + a_ab
        pw = _bf(a_ab)
        n = 1
        while 2 * n < l_c:
            p2 = _dot(pw, pw)
            pw = _bf(p2)
            tinv = tinv + _dot(_bf(tinv), pw)
            n *= 2

        s_bf = _bf(state[si])
        x = _dot_nt(ah, s_bf) + _dot(_bf(a_ak), vs)
        u = _bf(_dot(_bf(tinv), _bf(x)))
        y = _dot_nt(rh, s_bf) + _dot(_bf(a_rb), u) + _dot(_bf(a_rk), vs)
        upd = _dot_tn(u, bt) + _dot_tn(vs, kt)
        state[si] = state[si] * jnp.exp(c_last) + jnp.where(m_state, upd, 0.0)
        yq = y[0:l_c]
        for hh in range(1, HEADS_PER_GROUP):
            yq = yq + y[hh * l_c:(hh + 1) * l_c]
        y_ref[bi, :, sl] = yq[0:l_in]

    @pl.when(c_idx == pl.num_programs(1) - 1)
    def _():
        for bi, g in units:
            for hh in range(HEADS_PER_GROUP):
                lo = hh * HEAD_DIM
                sout_ref[bi, g * HEADS_PER_GROUP + hh] = (
                    state[bi * n_groups + g, lo:lo + HEAD_DIM, lo:lo + HEAD_DIM])


def _rwkv_chunk(r, lw, k, v, kk, a, s0, *, l_in, l_c, valid):
    bsz, t, d = r.shape
    n_heads = d // HEAD_DIM
    n_groups = n_heads // HEADS_PER_GROUP
    gw = HEADS_PER_GROUP * HEAD_DIM
    n_seq = 2 if bsz % 2 == 0 else 1
    seq = pl.BlockSpec((n_seq, l_in, d), lambda b, c: (b, c, 0))
    st = pl.BlockSpec((n_seq, n_heads, HEAD_DIM, HEAD_DIM), lambda b, c: (b, 0, 0, 0))
    return pl.pallas_call(
        functools.partial(_rwkv_chunk_kernel, l_in=l_in, l_c=l_c, valid=valid, n_groups=n_groups,
                          n_seq=n_seq),
        grid=(bsz // n_seq, t // l_in),
        in_specs=[seq] * 6 + [st], out_specs=[seq, st],
        out_shape=[jax.ShapeDtypeStruct((bsz, t, d), F32),
                   jax.ShapeDtypeStruct((bsz, n_heads, HEAD_DIM, HEAD_DIM), F32)],
        scratch_shapes=[pltpu.VMEM((n_seq * n_groups, gw, gw), F32)],
        compiler_params=_params(("arbitrary", "arbitrary")),
    )(r, lw, k, v, kk, a, s0)


def _sb_proj_kernel(h_ref, npre_ref, w_ref, q_o, k_o, v_o, gz_o, kb_o, vb_o, *, d_inner, scale):
    xn = _bf(_rms(h_ref[...], npre_ref[...]))
    q = _dot(xn, w_ref[:, 0:d_inner])
    q_o[...] = _bf(q * scale)
    k = _dot(xn, w_ref[:, d_inner:2 * d_inner])
    k_o[...] = k
    kb_o[...] = _bf(k)
    v = _dot(xn, w_ref[:, 2 * d_inner:3 * d_inner])
    v_o[...] = v
    vb_o[...] = _bf(v)
    z = _dot(xn, w_ref[:, 3 * d_inner:4 * d_inner])
    gz_o[...] = z * _sigmoid(z)


def _sb_proj(h2d, npre, w_qkvz, *, tm):
    rows, d = h2d.shape
    d_inner = w_qkvz.shape[1] // 4
    tile = pl.BlockSpec((tm, d), lambda i: (i, 0))
    otile = pl.BlockSpec((tm, d_inner), lambda i: (i, 0))
    f = jax.ShapeDtypeStruct((rows, d_inner), F32)
    h = jax.ShapeDtypeStruct((rows, d_inner), BF16)
    return pl.pallas_call(
        functools.partial(_sb_proj_kernel, d_inner=d_inner, scale=HEAD_DIM ** -0.5),
        grid=(rows // tm,),
        in_specs=[tile, _const_spec((1, d)), _const_spec(w_qkvz.shape)],
        out_specs=[otile] * 6, out_shape=[h, f, f, f, h, h],
        compiler_params=_params(("arbitrary",)),
    )(h2d, npre.reshape(1, -1), w_qkvz)


def _sb_weights(l, carry, tri, mask):
    sp = jnp.maximum(l, 0.0) + jnp.log(1.0 + jnp.exp(-jnp.abs(l)))
    if mask is not None:
        sp = jnp.where(mask, sp, 0.0)
    cs = _dot(_bf(sp), tri)
    a = jnp.exp(l - carry - sp - cs)
    if mask is not None:
        a = jnp.where(mask, a, 0.0)
    return _bf(a), carry + jnp.sum(sp, axis=-1, keepdims=True)


def _later_key_matrix(n):
    return _bf(jnp.where(lax.broadcasted_iota(jnp.int32, (n, n), 0)
                         > lax.broadcasted_iota(jnp.int32, (n, n), 1), 1.0, 0.0))


def _sb_prompt_kernel(bias_ref, q_ref, k_ref, v_ref, o_ref, *, blk, q_blks):
    hp = pl.program_id(1)
    i = pl.program_id(2)
    heads_per_blk = LANES // HEAD_DIM
    rows = q_blks * blk
    lane_head = lax.broadcasted_iota(jnp.int32, (1, LANES), 1) // HEAD_DIM
    tri = _later_key_matrix(blk)
    q_row = lax.broadcasted_iota(jnp.int32, (rows, blk), 0)
    k_col = lax.broadcasted_iota(jnp.int32, (rows, blk), 1)
    q = q_ref[...]
    sels = [lane_head == hh for hh in range(heads_per_blk)]
    qs = [jnp.where(sel, q, jnp.zeros_like(q)) for sel in sels]
    biases = [bias_ref[hp * heads_per_blk + hh] for hh in range(heads_per_blk)]

    def visit(j, carries, mask):
        start = pl.multiple_of(j * blk, blk)
        kb = k_ref[pl.ds(start, blk), :]
        vb = v_ref[pl.ds(start, blk), :]
        o_new, new_carries = None, []
        for hh in range(heads_per_blk):
            a, c = _sb_weights(_dot_nt(qs[hh], kb) + biases[hh], carries[hh], tri, mask)
            part = _dot(a, jnp.where(sels[hh], vb, jnp.zeros_like(vb)))
            o_new = part if o_new is None else o_new + part
            new_carries.append(c)
        return o_new, tuple(new_carries)

    carries = (jnp.zeros((rows, 1), F32),) * heads_per_blk
    o_acc = None
    for d_blk in reversed(range(q_blks)):
        o_new, carries = visit(i * q_blks + d_blk, carries, k_col + d_blk * blk < q_row)
        o_acc = o_new if o_acc is None else o_acc + o_new

    def body(jj, st):
        o_acc, carries = st
        o_new, carries = visit(i * q_blks - 1 - jj, carries, None)
        return o_acc + o_new, carries

    o_acc, _ = lax.fori_loop(0, i * q_blks, body, (o_acc, carries))
    o_ref[...] = o_acc


def _sb_prompt(qb, kb, vb, bias, *, blk, q_blks):
    bsz, t, d = qb.shape
    rows = blk * q_blks
    grid = (bsz, d // LANES, t // rows)
    qspec = pl.BlockSpec((None, rows, LANES), lambda b, hp, i, *_: (b, i, hp))
    kvspec = pl.BlockSpec((None, t, LANES), lambda b, hp, i, *_: (b, 0, hp))
    return pl.pallas_call(
        functools.partial(_sb_prompt_kernel, blk=blk, q_blks=q_blks),
        grid_spec=pltpu.PrefetchScalarGridSpec(
            num_scalar_prefetch=0, grid=grid,
            in_specs=[pl.BlockSpec(memory_space=pltpu.SMEM), qspec, kvspec, kvspec],
            out_specs=qspec),
        out_shape=jax.ShapeDtypeStruct((bsz, t, d), F32),
        compiler_params=_params(("arbitrary", "arbitrary", "arbitrary")),
    )(bias, qb, kb, vb)


def _sb_sample_kernel(pt_ref, bias_ref, q_ref, kn_ref, vn_ref, *rest, n_heads, page, pages_per_step):
    kc_refs = rest[:pages_per_step]
    vc_refs = rest[pages_per_step:2 * pages_per_step]
    o_ref, qbd, acc, carry = rest[2 * pages_per_step:]
    j = pl.program_id(1)
    tq = SAMPLE_PAD
    rows = n_heads * tq
    d = n_heads * HEAD_DIM
    tri = _later_key_matrix(page)
    row_head = lax.broadcasted_iota(jnp.int32, (rows, 1), 0) // tq
    head_sel = row_head == (lax.broadcasted_iota(jnp.int32, (1, d), 1) // HEAD_DIM)

    @pl.when(j == 0)
    def _():
        q = q_ref[...]
        qbd[...] = _bf(jnp.where(head_sel, jnp.concatenate([q] * n_heads, axis=0), 0.0))
        kn = jnp.concatenate([_bf(kn_ref[...]), jnp.zeros((page - tq, d), BF16)], axis=0)
        vn = jnp.concatenate([_bf(vn_ref[...]), jnp.zeros((page - tq, d), BF16)], axis=0)
        row_t = lax.broadcasted_iota(jnp.int32, (rows, 1), 0) % tq
        mask = lax.broadcasted_iota(jnp.int32, (rows, page), 1) < row_t
        a, c_new = _sb_weights(_dot_nt(qbd[...], kn) + bias_ref[...], jnp.zeros((rows, 1), F32), tri, mask)
        acc[...] = _dot(a, vn)
        carry[...] = c_new

    q_all = qbd[...]
    c = carry[...]
    o_new = None
    for kc_ref, vc_ref in zip(kc_refs, vc_refs):
        a, c = _sb_weights(_dot(q_all, _bf(kc_ref[...])) + bias_ref[...], c, tri, None)
        part = _dot_nt(a, _bf(vc_ref[...]))
        o_new = part if o_new is None else o_new + part
    acc[...] += o_new
    carry[...] = c

    @pl.when(j == pl.num_programs(1) - 1)
    def _():
        full = jnp.where(head_sel, acc[...], 0.0)
        out = full[0:tq]
        for hh in range(1, n_heads):
            out = out + full[hh * tq:(hh + 1) * tq]
        o_ref[...] = out


def _sb_sample(page_table, bias, qb, k_new, v_new, cache_k, cache_v, layer):
    bsz, tq, d = qb.shape
    n_heads = d // HEAD_DIM
    n_pages = page_table.shape[1]
    page = cache_k.shape[3]
    rows = n_heads * tq
    pt = page_table.reshape(-1)
    bias_rows = jnp.repeat(bias.astype(F32), tq).reshape(rows, 1)
    tok = pl.BlockSpec((None, tq, d), lambda b, j, pt: (b, 0, 0))
    pps = max(p for p in (1, 2, 4) if n_pages % p == 0)

    def page_spec(p):
        return pl.BlockSpec(
            (None, None, d, page),
            lambda b, j, pt: (layer, pt[b * n_pages + n_pages - 1 - (j * pps + p)], 0, 0))

    pages = [page_spec(p) for p in range(pps)]
    return pl.pallas_call(
        functools.partial(_sb_sample_kernel, n_heads=n_heads, page=page, pages_per_step=pps),
        grid_spec=pltpu.PrefetchScalarGridSpec(
            num_scalar_prefetch=1, grid=(bsz, n_pages // pps),
            in_specs=[pl.BlockSpec((rows, 1), lambda b, j, pt: (0, 0)), tok, tok, tok] + pages + pages,
            out_specs=tok,
            scratch_shapes=[pltpu.VMEM((rows, d), BF16), pltpu.VMEM((rows, d), F32),
                            pltpu.VMEM((rows, 1), F32)]),
        out_shape=jax.ShapeDtypeStruct((bsz, tq, d), F32),
        compiler_params=_params(("arbitrary", "arbitrary")),
    )(pt, bias_rows, qb, k_new, v_new, *([cache_k] * pps), *([cache_v] * pps))


def _tail_kernel(*refs, rwkv):
    it = iter(refs)
    h_ref = next(it); y_ref = next(it); gz_ref = next(it)
    if rwkv:
        bonus_ref = next(it); lnw_ref = next(it); lnb_ref = next(it); g_ref = next(it)
    p_ref = next(it); wout_ref = next(it); npost_ref = next(it); wple_ref = next(it); wgate_ref = next(it)
    o_ref = next(it)
    y = y_ref[...]
    if rwkv:
        inv_n = 1.0 / HEAD_DIM
        mu = _dot_split(y, g_ref[...]) * inv_n
        dlt = y - mu
        var = _dot_split(dlt * dlt, g_ref[...]) * inv_n
        y = dlt * lax.rsqrt(var + GN_EPS) * lnw_ref[...] + lnb_ref[...] + bonus_ref[...]
    br = y * gz_ref[...]
    h1 = h_ref[...] + _rms(_dot(_bf(br), wout_ref[...]), npost_ref[...])
    gate = _sigmoid(_dot(_bf(h1), wgate_ref[...]))
    o_ref[...] = h1 + _dot(_bf(p_ref[...]), wple_ref[...]) * gate


def _tail(h2d, y, gz, rwkv_extra, p2d, w_out, npost, w_ple, w_gate, *, tm):
    rows, d = h2d.shape
    tile = pl.BlockSpec((tm, d), lambda i: (i, 0))
    rwkv = rwkv_extra is not None
    ins = [h2d, y, gz]
    specs = [tile, tile, tile]
    if rwkv:
        bonus, lnw, lnb, gmat = rwkv_extra
        ins += [bonus, lnw.reshape(1, -1), lnb.reshape(1, -1), gmat]
        specs += [tile, _const_spec((1, d)), _const_spec((1, d)), _const_spec(gmat.shape)]
    ins += [p2d, w_out, npost.reshape(1, -1), w_ple, w_gate]
    specs += [pl.BlockSpec((tm, p2d.shape[1]), lambda i: (i, 0)), _const_spec(w_out.shape),
              _const_spec((1, d)), _const_spec(w_ple.shape), _const_spec(w_gate.shape)]
    return pl.pallas_call(
        functools.partial(_tail_kernel, rwkv=rwkv),
        grid=(rows // tm,), in_specs=specs, out_specs=tile,
        out_shape=jax.ShapeDtypeStruct((rows, d), F32),
        compiler_params=_params(("arbitrary",)),
    )(*ins)


def _row_tile(rows):
    for tm in (256, 128, 64, 32, 16, 8):
        if rows % tm == 0:
            return tm
    raise ValueError(f"row count {rows} is not a multiple of {SUBLANES}")


def kernel(x_prompt, x_sample, cache_k, cache_v, state_wkv, state_shift, page_table, p_prompt, p_sample, norm_pre, norm_post, w_out, w_ple, w_ple_gate, rwkv_mix, w_rkvz, w_decay0, w_decay1, w_decay2, w_a0, w_a1, w_a2, w_v0, w_v1, w_v2, k_k, k_a, r_k, lnx_w, lnx_b, w_qkvz, sb_bias):
    depth = norm_pre.shape[0]
    bp, tp, d = x_prompt.shape
    bs, ts, _ = x_sample.shape
    n_heads = d // HEAD_DIM
    n_sb, n_phys, page = cache_k.shape[:3]
    assert ts <= SAMPLE_PAD and tp % MXU_DIM == 0 and n_heads % HEADS_PER_GROUP == 0

    pad_t = lambda x: jnp.pad(x, [(0, 0)] * (x.ndim - 2) + [(0, SAMPLE_PAD - ts), (0, 0)])
    hp = x_prompt.reshape(bp * tp, d)
    hs = pad_t(x_sample).reshape(bs * SAMPLE_PAD, d)
    pp = p_prompt.reshape(depth, bp * tp, -1)
    ps = pad_t(p_sample).reshape(depth, bs * SAMPLE_PAD, -1)
    ck = jnp.transpose(cache_k, (0, 1, 3, 4, 2)).reshape(n_sb, n_phys, d, page)
    cv = jnp.transpose(cache_v, (0, 1, 3, 4, 2)).reshape(n_sb, n_phys, d, page)
    head_of = jnp.arange(d, dtype=jnp.int32) // HEAD_DIM
    gmat = (head_of[:, None] == head_of[None, :]).astype(BF16)
    tm_p = _row_tile(bp * tp)
    tm_s = _row_tile(bs * SAMPLE_PAD)
    l_chunk = HEAD_DIM
    blk = MXU_DIM

    k_p, v_p, k_s, v_s = [], [], [], []
    wkv_p, sh_p, wkv_s, sh_s = [], [], [], []
    vf_p = vf_s = None
    for i in range(depth):
        j = i // 2
        if i % 2 == 0:
            vres = None if j == 0 else (w_v0[j - 1], _bf(w_v1[j - 1]), _bf(w_v2[j - 1]))
            prm = (norm_pre[i], rwkv_mix[j], _bf(w_rkvz[j]), w_decay0[j], _bf(w_decay1[j]),
                   _bf(w_decay2[j]), w_a0[j], _bf(w_a1[j]), _bf(w_a2[j]), vres, k_k[j], k_a[j],
                   r_k[j].reshape(-1), gmat)
            r, lw, k, v, kkn, a, bonus, gz, xlast = _rwkv_proj(
                hp, jnp.zeros((SUBLANES, d), F32), vf_p, *prm, seq_rows=tp, tm=tm_p)
            sh_p.append(xlast.reshape(bp, SUBLANES, d)[:, -1])
            if j == 0:
                vf_p = v
            sq = lambda x: x.reshape(bp, tp, d)
            y, s_fin = _rwkv_chunk(sq(r), sq(lw), sq(k), sq(v), sq(kkn), sq(a),
                                   jnp.zeros((bp, n_heads, HEAD_DIM, HEAD_DIM), F32),
                                   l_in=l_chunk, l_c=l_chunk, valid=l_chunk)
            wkv_p.append(s_fin)
            hp = _tail(hp, y.reshape(bp * tp, d), gz, (bonus, lnx_w[j], lnx_b[j], gmat), pp[i],
                       _bf(w_out[i]), norm_post[i], _bf(w_ple[i]), _bf(w_ple_gate[i]), tm=tm_p)
            shift_rows = jnp.pad(state_shift[j][:, None, :],
                                 ((0, 0), (0, SAMPLE_PAD - 1), (0, 0))).reshape(bs * SAMPLE_PAD, d)
            r, lw, k, v, kkn, a, bonus, gz, xn = _rwkv_proj(
                hs, shift_rows, vf_s, *prm, seq_rows=SAMPLE_PAD, tm=tm_s)
            sh_s.append(xn.reshape(bs, SAMPLE_PAD, d)[:, ts - 1])
            if j == 0:
                vf_s = v
            sq = lambda x: x.reshape(bs, SAMPLE_PAD, d)
            y, s_fin = _rwkv_chunk(sq(r), sq(lw), sq(k), sq(v), sq(kkn), sq(a), state_wkv[j],
                                   l_in=SAMPLE_PAD, l_c=2 * SAMPLE_PAD, valid=ts)
            wkv_s.append(s_fin)
            hs = _tail(hs, y.reshape(bs * SAMPLE_PAD, d), gz, (bonus, lnx_w[j], lnx_b[j], gmat),
                       ps[i], _bf(w_out[i]), norm_post[i], _bf(w_ple[i]), _bf(w_ple_gate[i]), tm=tm_s)
        else:
            wq = _bf(w_qkvz[j])
            qb, k, v, gz, kb, vb = _sb_proj(hp, norm_pre[i], wq, tm=tm_p)
            k_p.append(k.reshape(bp, tp, n_heads, HEAD_DIM))
            v_p.append(v.reshape(bp, tp, n_heads, HEAD_DIM))
            sq = lambda x: x.reshape(bp, tp, d)
            o = _sb_prompt(sq(qb), sq(kb), sq(vb), sb_bias[j], blk=blk,
                           q_blks=2 if tp % (2 * blk) == 0 else 1)
            hp = _tail(hp, o.reshape(bp * tp, d), gz, None, pp[i],
                       _bf(w_out[i]), norm_post[i], _bf(w_ple[i]), _bf(w_ple_gate[i]), tm=tm_p)
            qb, k, v, gz, _, _ = _sb_proj(hs, norm_pre[i], wq, tm=tm_s)
            sq = lambda x: x.reshape(bs, SAMPLE_PAD, d)
            k_s.append(sq(k)[:, :ts].reshape(bs, ts, n_heads, HEAD_DIM))
            v_s.append(sq(v)[:, :ts].reshape(bs, ts, n_heads, HEAD_DIM))
            o = _sb_sample(page_table, sb_bias[j], sq(qb.astype(F32)), sq(k), sq(v), ck, cv, j)
            hs = _tail(hs, o.reshape(bs * SAMPLE_PAD, d), gz, None, ps[i],
                       _bf(w_out[i]), norm_post[i], _bf(w_ple[i]), _bf(w_ple_gate[i]), tm=tm_s)

    y_prompt = hp.reshape(bp, tp, d)
    y_sample = hs.reshape(bs, SAMPLE_PAD, d)[:, :ts]
    return (y_prompt, y_sample, jnp.stack(k_p), jnp.stack(v_p), jnp.stack(wkv_p), jnp.stack(sh_p),
            jnp.stack(k_s), jnp.stack(v_s), jnp.stack(wkv_s), jnp.stack(sh_s))
```
